```python
import jax, jax.numpy as jnp
from jax import lax
import numpy as np

D_MODEL = 1024
BATCH = 4
SEQ = 4096
DEPTH = 4
DEC_BATCH = 128
DEC_SEQ = 1
PAST_LEN = 8192
PAGE_SIZE = 128

HEAD_DIM = 64
N_HEADS = 12
N_KV_HEADS = 4
GROUP = N_HEADS // N_KV_HEADS
MEM_HEADS = 4
MEM_LEN = 256
ATT_W = N_HEADS * HEAD_DIM
KV_W = N_KV_HEADS * HEAD_DIM
MEM_W = MEM_HEADS * HEAD_DIM
MIX_W = ATT_W + MEM_W
ROT_DIM = HEAD_DIM // 4
ROPE_THETA = 500000.0
WINDOW = 128
BLOCK = 128
N_MIXERS = 2
N_FOX = (DEPTH + 1) // 2
N_SWA = DEPTH // 2
W_IN_SWA = ATT_W + 2 * KV_W + ATT_W + MEM_W + MEM_W
W_IN_FOX = W_IN_SWA + N_HEADS
FORGET_BIAS = 3.0
EPS = 1e-6
SCALE = HEAD_DIM ** -0.5

kernel_name = 'fox_swa_memory_hybrid_step'


def rmsnorm(x, g):
    xf = x.astype(jnp.float32)
    y = xf * lax.rsqrt(jnp.mean(xf * xf, axis=-1, keepdims=True) + EPS)
    return (y * g.astype(jnp.float32)).astype(x.dtype)


def rope_partial(x, pos):
    half = ROT_DIM // 2
    inv_freq = ROPE_THETA ** (-jnp.arange(0, ROT_DIM, 2, dtype=jnp.float32) / ROT_DIM)
    ang = pos.astype(jnp.float32)[:, None] * inv_freq[None, :]
    cos = jnp.cos(ang)[:, None, :]
    sin = jnp.sin(ang)[:, None, :]
    xf = x.astype(jnp.float32)
    x1 = xf[..., :half]
    x2 = xf[..., half:ROT_DIM]
    out = jnp.concatenate([x1 * cos - x2 * sin, x2 * cos + x1 * sin, xf[..., ROT_DIM:]], axis=-1)
    return out.astype(x.dtype)


def split_in(u, with_forget):
    sizes = [ATT_W, KV_W, KV_W, ATT_W, MEM_W, MEM_W] + ([N_HEADS] if with_forget else [])
    parts = jnp.split(u, np.cumsum(sizes)[:-1].tolist(), axis=-1)
    lead = u.shape[:-1]
    q = parts[0].reshape(*lead, N_HEADS, HEAD_DIM)
    k = parts[1].reshape(*lead, N_KV_HEADS, HEAD_DIM)
    v = parts[2].reshape(*lead, N_KV_HEADS, HEAD_DIM)
    q_mem = parts[4].reshape(*lead, MEM_HEADS, HEAD_DIM)
    fl = parts[6] if with_forget else None
    return q, k, v, parts[3], q_mem, parts[5], fl


def forget_log(fl, b):
    return jax.nn.log_sigmoid(fl.astype(jnp.float32) + b.astype(jnp.float32))


def gather_pages(pool, page_table):
    g = pool[page_table]
    return g.reshape(g.shape[0], g.shape[1] * g.shape[2], *g.shape[3:])


def fox_prompt(q, k, v, logf):
    B, S = q.shape[:2]
    nb = S // BLOCK
    c = jnp.cumsum(logf, axis=1)
    c_keys = c.reshape(B, S, N_KV_HEADS, GROUP).transpose(0, 2, 3, 1)[:, :, :, None, :]
    qb = q.reshape(B, nb, BLOCK, N_KV_HEADS, GROUP, HEAD_DIM).transpose(1, 0, 2, 3, 4, 5)
    cb = c.reshape(B, nb, BLOCK, N_KV_HEADS, GROUP).transpose(1, 0, 3, 4, 2)
    kpos = jnp.arange(S)

    def one_block(args):
        qi, ci, bi = args
        s = jnp.einsum('bqkgd,bskd->bkgqs', qi, k).astype(jnp.float32) * SCALE
        s = s + (ci[..., None] - c_keys)
        qpos = bi * BLOCK + jnp.arange(BLOCK)
        mask = kpos[None, :] <= qpos[:, None]
        s = jnp.where(mask, s, -jnp.inf)
        p = jax.nn.softmax(s, axis=-1).astype(v.dtype)
        return jnp.einsum('bkgqs,bskd->bqkgd', p, v)

    o = lax.map(one_block, (qb, cb, jnp.arange(nb)))
    return o.transpose(1, 0, 2, 3, 4, 5).reshape(B, S, ATT_W)


def fox_sample(q, k, v, logf, k_past, v_past, logf_past):
    Bd, T = q.shape[:2]
    P = k_past.shape[1]
    q5 = q.reshape(Bd, T, N_KV_HEADS, GROUP, HEAD_DIM)
    c_past = jnp.cumsum(logf_past.astype(jnp.float32), axis=1)
    c_new = c_past[:, -1:] + jnp.cumsum(logf, axis=1)
    cq = c_new.reshape(Bd, T, N_KV_HEADS, GROUP).transpose(0, 2, 3, 1)[..., None]
    ck_past = c_past.reshape(Bd, P, N_KV_HEADS, GROUP).transpose(0, 2, 3, 1)[:, :, :, None, :]
    ck_new = c_new.reshape(Bd, T, N_KV_HEADS, GROUP).transpose(0, 2, 3, 1)[:, :, :, None, :]
    s_past = jnp.einsum('btkgd,bskd->bkgts', q5, k_past).astype(jnp.float32) * SCALE + (cq - ck_past)
    s_new = jnp.einsum('btkgd,bskd->bkgts', q5, k).astype(jnp.float32) * SCALE + (cq - ck_new)
    causal = jnp.arange(T)[None, :] <= jnp.arange(T)[:, None]
    s_new = jnp.where(causal, s_new, -jnp.inf)
    p = jax.nn.softmax(jnp.concatenate([s_past, s_new], axis=-1), axis=-1).astype(v.dtype)
    o = (jnp.einsum('bkgts,bskd->btkgd', p[..., :P], v_past)
         + jnp.einsum('bkgts,bskd->btkgd', p[..., P:], v))
    return o.reshape(Bd, T, ATT_W)


def swa_prompt(q, k, v, sinks):
    B, S = q.shape[:2]
    nb = S // BLOCK
    qb = q.reshape(B, nb, BLOCK, N_KV_HEADS, GROUP, HEAD_DIM)
    kb = k.reshape(B, nb, BLOCK, N_KV_HEADS, HEAD_DIM)
    vb = v.reshape(B, nb, BLOCK, N_KV_HEADS, HEAD_DIM)
    kc = jnp.concatenate([jnp.concatenate([jnp.zeros_like(kb[:, :1]), kb[:, :-1]], axis=1), kb], axis=2)
    vc = jnp.concatenate([jnp.concatenate([jnp.zeros_like(vb[:, :1]), vb[:, :-1]], axis=1), vb], axis=2)
    s = jnp.einsum('bnqkgd,bnskd->bnkgqs', qb, kc).astype(jnp.float32) * SCALE
    i = jnp.arange(BLOCK)[:, None]
    jj = jnp.arange(2 * BLOCK)[None, :]
    rel = BLOCK + i - jj
    n = jnp.arange(nb)[:, None, None]
    valid = (rel >= 0)[None] & (rel <= WINDOW)[None] & ((n > 0) | (jj >= BLOCK)[None])
    s = jnp.where(valid[None, :, None, None], s, -jnp.inf)
    sk = sinks.astype(jnp.float32).reshape(N_KV_HEADS, GROUP)[None, None, :, :, None, None]
    sk = jnp.broadcast_to(sk, s.shape[:-1] + (1,))
    p = jax.nn.softmax(jnp.concatenate([s, sk], axis=-1), axis=-1)[..., :-1].astype(v.dtype)
    o = jnp.einsum('bnkgqs,bnskd->bnqkgd', p, vc)
    return o.reshape(B, S, ATT_W)


def swa_sample(q, k, v, sinks, k_buf, v_buf):
    Bd, T = q.shape[:2]
    Lb = k_buf.shape[1]
    q5 = q.reshape(Bd, T, N_KV_HEADS, GROUP, HEAD_DIM)
    s_buf = jnp.einsum('btkgd,bskd->bkgts', q5, k_buf).astype(jnp.float32) * SCALE
    s_new = jnp.einsum('btkgd,bskd->bkgts', q5, k).astype(jnp.float32) * SCALE
    jt = jnp.arange(T)[:, None]
    valid_buf = (jt + Lb - jnp.arange(Lb)[None, :]) <= WINDOW
    rel = jt - jnp.arange(T)[None, :]
    valid_new = (rel >= 0) & (rel <= WINDOW)
    s_buf = jnp.where(valid_buf, s_buf, -jnp.inf)
    s_new = jnp.where(valid_new, s_new, -jnp.inf)
    sk = sinks.astype(jnp.float32).reshape(N_KV_HEADS, GROUP)[None, :, :, None, None]
    sk = jnp.broadcast_to(sk, s_new.shape[:-1] + (1,))
    p = jax.nn.softmax(jnp.concatenate([s_buf, s_new, sk], axis=-1), axis=-1).astype(v.dtype)
    o = (jnp.einsum('bkgts,bskd->btkgd', p[..., :Lb], v_buf)
         + jnp.einsum('bkgts,bskd->btkgd', p[..., Lb:Lb + T], v))
    k_new_buf = jnp.concatenate([k_buf, k], axis=1)[:, -Lb:]
    v_new_buf = jnp.concatenate([v_buf, v], axis=1)[:, -Lb:]
    return o.reshape(Bd, T, ATT_W), k_new_buf, v_new_buf


def memory_kv(mem, g, w):
    u = rmsnorm(mem, g) @ w
    mk, mv = jnp.split(u, 2, axis=-1)
    lead = mem.shape[:-1]
    return mk.reshape(*lead, MEM_HEADS, HEAD_DIM), mv.reshape(*lead, MEM_HEADS, HEAD_DIM)


def memory_attention(q, mk, mv):
    s = jnp.einsum('bqhd,bkhd->bhqk', q, mk).astype(jnp.float32) * SCALE
    p = jax.nn.softmax(s, axis=-1).astype(mv.dtype)
    o = jnp.einsum('bhqk,bkhd->bqhd', p, mv)
    return o.reshape(*q.shape[:2], MEM_W)


def gate_out(x, a, z_a, m, z_m, w, g_post):
    y = jnp.concatenate([a * jax.nn.silu(z_a), m * jax.nn.silu(z_m)], axis=-1) @ w
    return x + rmsnorm(y, g_post)


def setup_inputs(seed: int = 0) -> dict:
    key = jax.random.key(seed)
    ks = jax.random.split(key, 24)
    n_pages = PAST_LEN // PAGE_SIZE
    n_used = DEC_BATCH * n_pages
    n_pool = n_used + n_used // 4
    swa_buf = min(WINDOW, PAST_LEN)
    f32 = jnp.float32
    page_table = jax.random.permutation(ks[0], n_pool)[:n_used].reshape(DEC_BATCH, n_pages).astype(jnp.int32)
    return {
        'x_prompt': jax.random.normal(ks[1], (BATCH, SEQ, D_MODEL), f32),
        'x_sample': jax.random.normal(ks[2], (DEC_BATCH, DEC_SEQ, D_MODEL), f32),
        'mem_prompt': jax.random.normal(ks[3], (BATCH, MEM_LEN, D_MODEL), f32),
        'cache_fox_k': jax.random.normal(ks[4], (N_FOX, n_pool, PAGE_SIZE, N_KV_HEADS, HEAD_DIM), f32),
        'cache_fox_v': jax.random.normal(ks[5], (N_FOX, n_pool, PAGE_SIZE, N_KV_HEADS, HEAD_DIM), f32),
        'cache_fox_logf': jax.nn.log_sigmoid(FORGET_BIAS + jax.random.normal(ks[6], (N_FOX, n_pool, PAGE_SIZE, N_HEADS), f32)),
        'cache_swa_k': jax.random.normal(ks[7], (N_SWA, DEC_BATCH, swa_buf, N_KV_HEADS, HEAD_DIM), f32),
        'cache_swa_v': jax.random.normal(ks[8], (N_SWA, DEC_BATCH, swa_buf, N_KV_HEADS, HEAD_DIM), f32),
        'cache_mem_k': jax.random.normal(ks[9], (DEPTH, DEC_BATCH, MEM_LEN, MEM_HEADS, HEAD_DIM), f32),
        'cache_mem_v': jax.random.normal(ks[10], (DEPTH, DEC_BATCH, MEM_LEN, MEM_HEADS, HEAD_DIM), f32),
        'page_table': page_table,
        'norm_pre': 1.0 + 0.05 * jax.random.normal(ks[11], (DEPTH, D_MODEL), f32),
        'norm_post': 1.0 + 0.05 * jax.random.normal(ks[12], (DEPTH, D_MODEL), f32),
        'norm_mem': 1.0 + 0.05 * jax.random.normal(ks[13], (DEPTH, D_MODEL), f32),
        'w_in_fox': jax.random.normal(ks[14], (N_FOX, D_MODEL, W_IN_FOX), f32) * D_MODEL ** -0.5,
        'b_forget': FORGET_BIAS + 0.1 * jax.random.normal(ks[15], (N_FOX, N_HEADS), f32),
        'w_in_swa': jax.random.normal(ks[16], (N_SWA, D_MODEL, W_IN_SWA), f32) * D_MODEL ** -0.5,
        'sinks': jax.random.normal(ks[17], (N_SWA, N_HEADS), f32),
        'w_mem_kv': jax.random.normal(ks[18], (DEPTH, D_MODEL, 2 * MEM_W), f32) * D_MODEL ** -0.5,
        'w_out': jax.random.normal(ks[19], (DEPTH, MIX_W, D_MODEL), f32) * MIX_W ** -0.5,
    }


def reference(x_prompt, x_sample, mem_prompt, cache_fox_k, cache_fox_v, cache_fox_logf,
              cache_swa_k, cache_swa_v, cache_mem_k, cache_mem_v, page_table,
              norm_pre, norm_post, norm_mem, w_in_fox, b_forget, w_in_swa, sinks,
              w_mem_kv, w_out):
    pos_p = jnp.arange(SEQ)
    pos_s = PAST_LEN + jnp.arange(DEC_SEQ)
    buf_p = min(WINDOW, SEQ)
    xp, xs = x_prompt, x_sample
    fox_kp, fox_vp, fox_fp, fox_ks, fox_vs, fox_fs = [], [], [], [], [], []
    swa_kp, swa_vp, swa_ks, swa_vs = [], [], [], []
    mem_kp, mem_vp = [], []
    for l in range(DEPTH):
        j = l // N_MIXERS
        hp = rmsnorm(xp, norm_pre[l])
        hs = rmsnorm(xs, norm_pre[l])
        mk_p, mv_p = memory_kv(mem_prompt, norm_mem[l], w_mem_kv[l])
        mem_kp.append(mk_p)
        mem_vp.append(mv_p)
        if l % N_MIXERS == 0:
            qp, kp, vp, zap, qmp, zmp, flp = split_in(hp @ w_in_fox[j], True)
            qs, ks_, vs, zas, qms, zms, fls = split_in(hs @ w_in_fox[j], True)
            lfp = forget_log(flp, b_forget[j])
            lfs = forget_log(fls, b_forget[j])
            ap = fox_prompt(qp, kp, vp, lfp)
            as_ = fox_sample(qs, ks_, vs, lfs,
                             gather_pages(cache_fox_k[j], page_table),
                             gather_pages(cache_fox_v[j], page_table),
                             gather_pages(cache_fox_logf[j], page_table))
            fox_kp.append(kp)
            fox_vp.append(vp)
            fox_fp.append(lfp)
            fox_ks.append(ks_)
            fox_vs.append(vs)
            fox_fs.append(lfs)
        else:
            qp, kp, vp, zap, qmp, zmp, _ = split_in(hp @ w_in_swa[j], False)
            qs, ks_, vs, zas, qms, zms, _ = split_in(hs @ w_in_swa[j], False)
            qp = rope_partial(qp, pos_p)
            kp = rope_partial(kp, pos_p)
            qs = rope_partial(qs, pos_s)
            ks_ = rope_partial(ks_, pos_s)
            ap = swa_prompt(qp, kp, vp, sinks[j])
            as_, kbuf, vbuf = swa_sample(qs, ks_, vs, sinks[j], cache_swa_k[j], cache_swa_v[j])
            swa_kp.append(kp[:, -buf_p:])
            swa_vp.append(vp[:, -buf_p:])
            swa_ks.append(kbuf)
            swa_vs.append(vbuf)
        mp = memory_attention(qmp, mk_p, mv_p)
        ms = memory_attention(qms, cache_mem_k[l], cache_mem_v[l])
        xp = gate_out(xp, ap, zap, mp, zmp, w_out[l], norm_post[l])
        xs = gate_out(xs, as_, zas, ms, zms, w_out[l], norm_post[l])
    return (xp, xs,
            jnp.stack(fox_kp), jnp.stack(fox_vp), jnp.stack(fox_fp),
            jnp.stack(swa_kp), jnp.stack(swa_vp),
            jnp.stack(mem_kp), jnp.stack(mem_vp),
            jnp.stack(fox_ks), jnp.stack(fox_vs), jnp.stack(fox_fs),
            jnp.stack(swa_ks), jnp.stack(swa_vs))
```

```python
import functools

import numpy as np
import jax
import jax.numpy as jnp
from jax import lax
from jax.experimental import pallas as pl
from jax.experimental.pallas import tpu as pltpu

D_MODEL = 1024
HEAD_DIM = 64
N_HEADS = 12
N_KV_HEADS = 4
GROUP = N_HEADS // N_KV_HEADS
MEM_HEADS = 4
ATT_W = N_HEADS * HEAD_DIM
KV_W = N_KV_HEADS * HEAD_DIM
MEM_W = MEM_HEADS * HEAD_DIM
ROT_DIM = HEAD_DIM // 4
ROPE_THETA = 500000.0
WINDOW = 128
PAGE_SIZE = 128
EPS = 1e-6
SCALE = HEAD_DIM ** -0.5

LANE = 128
SUBLANE = 8
VMEM_LIMIT = 56 * 1024 * 1024
NEG = -1e30

C_Q, C_K, C_V, C_Z, C_QM, C_ZM, C_FL = 0, 768, 1024, 1280, 2048, 2304, 2560
W_IN_PAD = C_FL + LANE
ROWS_J = GROUP * SUBLANE

F32 = jnp.float32
BF16 = jnp.bfloat16
NT_DIMS = (((1,), (1,)), ((), ()))


def _cparams(*sem):
    return pltpu.CompilerParams(dimension_semantics=sem, vmem_limit_bytes=VMEM_LIMIT)


def _silu(z):
    return z / (1.0 + jnp.exp(-z))


def _log_sigmoid(x):
    return jnp.minimum(x, 0.0) - jnp.log(1.0 + jnp.exp(-jnp.abs(x)))


def _split3(x):
    hi = x.astype(BF16)
    r = x - hi.astype(F32)
    mid = r.astype(BF16)
    lo = (r - mid.astype(F32)).astype(BF16)
    return hi, mid, lo


def _dot(a, b):
    return jnp.dot(a, b, preferred_element_type=F32)


def _dot_nt(a, b):
    return lax.dot_general(a, b, NT_DIMS, preferred_element_type=F32)


def _head_mask(dtype, width=KV_W):
    lane = lax.broadcasted_iota(jnp.int32, (N_KV_HEADS, 1, width), 2)
    g = lax.broadcasted_iota(jnp.int32, (N_KV_HEADS, 1, width), 0)
    return jnp.where(lane // HEAD_DIM == g, 1.0, 0.0).astype(dtype)


def _rope(x, cos, s1, s2):
    cols = []
    for c in range(x.shape[1] // LANE):
        xs = x[:, c * LANE:(c + 1) * LANE]
        half = ROT_DIM // 2
        cols.append(xs * cos + pltpu.roll(xs, half, 1) * s1 + pltpu.roll(xs, LANE - half, 1) * s2)
    return jnp.concatenate(cols, axis=1)


def _inproj_body(*refs, forget, cumsum, rope, tiles_per_seq):
    it = iter(refs)
    x_ref, g_ref, w_ref = next(it), next(it), next(it)
    if rope:
        cos_ref, s1_ref, s2_ref = next(it), next(it), next(it)
    if forget:
        bf_ref = next(it)
    if cumsum:
        wft_ref, bft_ref, uo_ref = next(it), next(it), next(it)
    q_ref, k32_ref, v32_ref, kb_ref, vb_ref, sz_ref, qm_ref, szm_ref = (next(it) for _ in range(8))
    if forget:
        lf_ref = next(it)
    if cumsum:
        negc_ref = next(it)
        carry_ref = next(it)

    x = x_ref[...]
    h = x * lax.rsqrt(jnp.mean(x * x, axis=-1, keepdims=True) + EPS) * g_ref[...]
    hb = h.astype(BF16)

    def proj(a, b):
        return _dot(hb, w_ref[:, a:b])

    q = proj(C_Q, C_K)
    k = proj(C_K, C_V)
    v = proj(C_V, C_Z)
    if rope:
        cos, s1, s2 = cos_ref[...], s1_ref[...], s2_ref[...]
        q = _rope(q, cos, s1, s2)
        k = _rope(k, cos, s1, s2)
    q_ref[...] = q.astype(q_ref.dtype)
    k32_ref[...] = k
    v32_ref[...] = v
    kb_ref[...] = k.astype(BF16)
    vb_ref[...] = v.astype(BF16)
    sz_ref[...] = _silu(proj(C_Z, C_QM))
    qm_ref[...] = proj(C_QM, C_ZM).astype(BF16)
    szm_ref[...] = _silu(proj(C_ZM, C_FL))
    if forget:
        lf_ref[...] = _log_sigmoid(proj(C_FL, W_IN_PAD) + bf_ref[...])
    if cumsum:
        tm = x.shape[0]
        lft = _log_sigmoid(_dot_nt(wft_ref[...], hb) + bft_ref[...])
        hi, mid, lo = _split3(lft)
        uo = uo_ref[...]
        r = _dot(hi, uo) + _dot(mid, uo) + _dot(lo, uo)

        @pl.when(pl.program_id(0) % tiles_per_seq == 0)
        def _():
            carry_ref[...] = jnp.zeros_like(carry_ref)

        carry = carry_ref[...]
        negc_ref[...] = (-(r[:, :tm] + carry)).reshape(GROUP, SUBLANE, tm)
        carry_ref[...] = carry + r[:, tm:]


def _inproj(x, g, w, *, tm, seq, q_dtype, rope_tabs=None, b_forget=None, cum=None):
    n = x.shape[0]
    forget = b_forget is not None
    cumsum = cum is not None
    rope = rope_tabs is not None
    tps = seq // tm
    row = lambda i: (i, 0)
    const = lambda i: (0, 0)
    in_specs = [pl.BlockSpec((tm, D_MODEL), row), pl.BlockSpec((1, D_MODEL), const),
                pl.BlockSpec(w.shape, const)]
    args = [x, g.reshape(1, D_MODEL), w]
    if rope:
        tab_rows = rope_tabs[0].shape[0]
        tmap = (lambda i: (i % tps, 0)) if tab_rows > tm else const
        in_specs += [pl.BlockSpec((tm, LANE), tmap)] * 3
        args += list(rope_tabs)
    if forget:
        in_specs.append(pl.BlockSpec((1, LANE), const))
        args.append(b_forget)
    if cumsum:
        wft, bft, uo = cum
        in_specs += [pl.BlockSpec(wft.shape, const), pl.BlockSpec(bft.shape, const),
                     pl.BlockSpec(uo.shape, const)]
        args += [wft, bft, uo]
    widths = [(ATT_W, q_dtype), (KV_W, F32), (KV_W, F32), (KV_W, BF16), (KV_W, BF16),
              (ATT_W, F32), (MEM_W, BF16), (MEM_W, F32)]
    if forget:
        widths.append((LANE, F32))
    out_shape = [jax.ShapeDtypeStruct((n, wd), dt) for wd, dt in widths]
    out_specs = [pl.BlockSpec((tm, wd), row) for wd, _ in widths]
    scratch = []
    if cumsum:
        nb = n // seq
        out_shape.append(jax.ShapeDtypeStruct((nb, GROUP, SUBLANE, seq), F32))
        out_specs.append(pl.BlockSpec((None, GROUP, SUBLANE, tm), lambda i: (i // tps, 0, 0, i % tps)))
        scratch.append(pltpu.VMEM((ROWS_J, tm), F32))
    body = functools.partial(_inproj_body, forget=forget, cumsum=cumsum, rope=rope, tiles_per_seq=tps)
    return pl.pallas_call(
        body, grid=(n // tm,), in_specs=in_specs, out_specs=out_specs, out_shape=out_shape,
        scratch_shapes=scratch, compiler_params=_cparams("arbitrary"), name="inproj")(*args)


def _memkv_body(x_ref, g_ref, w_ref, k32_ref, v32_ref, kb_ref, vb_ref):
    x = x_ref[...]
    h = x * lax.rsqrt(jnp.mean(x * x, axis=-1, keepdims=True) + EPS) * g_ref[...]
    u = _dot(h.astype(BF16), w_ref[...])
    k, v = u[:, :MEM_W], u[:, MEM_W:]
    k32_ref[...] = k
    v32_ref[...] = v
    kb_ref[...] = k.astype(BF16)
    vb_ref[...] = v.astype(BF16)


def _memkv(mem, g, w, *, tm):
    n = mem.shape[0]
    depth = w.shape[0]
    spec_o = pl.BlockSpec((None, tm, MEM_W), lambda l, i: (l, i, 0))
    return pl.pallas_call(
        _memkv_body, grid=(depth, n // tm),
        in_specs=[pl.BlockSpec((tm, D_MODEL), lambda l, i: (i, 0)),
                  pl.BlockSpec((None, 1, D_MODEL), lambda l, i: (l, 0, 0)),
                  pl.BlockSpec((None, D_MODEL, 2 * MEM_W), lambda l, i: (l, 0, 0))],
        out_specs=[spec_o] * 4,
        out_shape=[jax.ShapeDtypeStruct((depth, n, MEM_W), dt) for dt in (F32, F32, BF16, BF16)],
        compiler_params=_cparams("arbitrary", "arbitrary"), name="memkv")(mem, g.reshape(depth, 1, D_MODEL), w)


def _stack_heads(q, hm):
    return jnp.concatenate([q * hm[g] for g in range(N_KV_HEADS)], axis=0)


def _unstack_heads(o, hm, rows):
    out = o[:rows] * hm[0]
    for g in range(1, N_KV_HEADS):
        out = out + o[g * rows:(g + 1) * rows] * hm[g]
    return out


def _fox_flash_body(q_ref, k_ref, v_ref, nc_ref, o_ref, q4_ref, m_ref, l_ref, acc_ref, *, tq, tk):
    qi = pl.program_id(2)
    q0 = qi * tq
    q4_ref[...] = _stack_heads(q_ref[...], _head_mask(BF16))
    m_ref[...] = jnp.full_like(m_ref, NEG)
    l_ref[...] = jnp.zeros_like(l_ref)
    acc_ref[...] = jnp.zeros_like(acc_ref)

    def step(k0, masked):
        kt = k_ref[pl.ds(k0, tk), :]
        vt = v_ref[pl.ds(k0, tk), :]
        s = _dot_nt(q4_ref[...], kt).reshape(N_KV_HEADS, tq, tk)
        s = s + nc_ref[0:N_KV_HEADS, pl.ds(k0, tk)][:, None, :]
        if masked:
            row = q0 + lax.broadcasted_iota(jnp.int32, (tq, tk), 0)
            col = k0 + lax.broadcasted_iota(jnp.int32, (tq, tk), 1)
            s = jnp.where((col <= row)[None], s, NEG)
        s = s.reshape(N_KV_HEADS * tq, tk)
        m_old = m_ref[...]
        m_new = jnp.maximum(m_old, jnp.max(s, axis=-1, keepdims=True))
        alpha = jnp.exp(m_old - m_new)
        p = jnp.exp(s - m_new)
        l_ref[...] = alpha * l_ref[...] + jnp.sum(p, axis=-1, keepdims=True)
        acc_ref[...] = alpha * acc_ref[...] + _dot(p.astype(BF16), vt)
        m_ref[...] = m_new

    def full_tile(i, carry):
        step(pl.multiple_of(i * tk, tk), False)
        return carry

    lax.fori_loop(0, q0 // tk, full_tile, 0)
    for r in range(tq // tk):
        step(pl.multiple_of(q0 + r * tk, tk), True)
    o_ref[...] = _unstack_heads(acc_ref[...] / l_ref[...], _head_mask(F32), tq)


def _fox_flash(q, kb, vb, negc, *, seq, tq, tk):
    n = q.shape[0]
    nb = n // seq
    nq = seq // tq
    qmap = lambda b, j, i: (b * nq + i, j)
    kvmap = lambda b, j, i: (b, 0)
    body = functools.partial(_fox_flash_body, tq=tq, tk=tk)
    return pl.pallas_call(
        body, grid=(nb, GROUP, nq),
        in_specs=[pl.BlockSpec((tq, KV_W), qmap), pl.BlockSpec((seq, KV_W), kvmap),
                  pl.BlockSpec((seq, KV_W), kvmap),
                  pl.BlockSpec((None, None, SUBLANE, seq), lambda b, j, i: (b, j, 0, 0))],
        out_specs=pl.BlockSpec((tq, KV_W), qmap),
        out_shape=jax.ShapeDtypeStruct((n, ATT_W), F32),
        scratch_shapes=[pltpu.VMEM((N_KV_HEADS * tq, KV_W), BF16), pltpu.VMEM((N_KV_HEADS * tq, 1), F32),
                        pltpu.VMEM((N_KV_HEADS * tq, 1), F32), pltpu.VMEM((N_KV_HEADS * tq, KV_W), F32)],
        compiler_params=_cparams("arbitrary", "arbitrary", "arbitrary"), name="fox_flash")(q, kb, vb, negc)


def _swa_prompt_body(q_ref, k_ref, v_ref, sink_ref, o_ref, *, tq):
    q0 = pl.program_id(2) * tq
    hmb = _head_mask(BF16)
    hmf = _head_mask(F32)
    blk = WINDOW
    sk = sink_ref[0:N_KV_HEADS, 0:1][:, :, None]
    for r in range(tq // blk):
        t0 = q0 + r * blk
        ks = pl.multiple_of(jnp.maximum(t0 - blk, 0), blk)
        q4 = _stack_heads(q_ref[r * blk:(r + 1) * blk, :], hmb)
        kt = k_ref[pl.ds(ks, 2 * blk), :]
        vt = v_ref[pl.ds(ks, 2 * blk), :]
        s = _dot_nt(q4, kt).reshape(N_KV_HEADS, blk, 2 * blk)
        row = t0 + lax.broadcasted_iota(jnp.int32, (blk, 2 * blk), 0)
        col = ks + lax.broadcasted_iota(jnp.int32, (blk, 2 * blk), 1)
        valid = (col <= row) & (row - col <= WINDOW)
        s = jnp.where(valid[None], s, NEG)
        m = jnp.maximum(jnp.max(s, axis=-1, keepdims=True), sk)
        p = jnp.exp(s - m)
        l = jnp.sum(p, axis=-1, keepdims=True) + jnp.exp(sk - m)
        pv = _dot(p.reshape(N_KV_HEADS * blk, 2 * blk).astype(BF16), vt)
        o = pv / l.reshape(N_KV_HEADS * blk, 1)
        o_ref[r * blk:(r + 1) * blk, :] = _unstack_heads(o, hmf, blk)


def _swa_prompt(q, kb, vb, sink_rows, *, seq, tq):
    n = q.shape[0]
    nb = n // seq
    nq = seq // tq
    qmap = lambda b, j, i: (b * nq + i, j)
    kvmap = lambda b, j, i: (b, 0)
    return pl.pallas_call(
        functools.partial(_swa_prompt_body, tq=tq), grid=(nb, GROUP, nq),
        in_specs=[pl.BlockSpec((tq, KV_W), qmap), pl.BlockSpec((seq, KV_W), kvmap),
                  pl.BlockSpec((seq, KV_W), kvmap),
                  pl.BlockSpec((None, SUBLANE, LANE), lambda b, j, i: (j, 0, 0))],
        out_specs=pl.BlockSpec((tq, KV_W), qmap),
        out_shape=jax.ShapeDtypeStruct((n, ATT_W), F32),
        compiler_params=_cparams("arbitrary", "arbitrary", "arbitrary"), name="swa_prompt")(q, kb, vb, sink_rows)


def _mem_prompt_body(q_ref, k_ref, v_ref, o_ref):
    tq = q_ref.shape[0]
    s = _dot_nt(_stack_heads(q_ref[...], _head_mask(BF16)), k_ref[...])
    m = jnp.max(s, axis=-1, keepdims=True)
    p = jnp.exp(s - m)
    l = jnp.sum(p, axis=-1, keepdims=True)
    o = _dot(p.astype(BF16), v_ref[...]) / l
    o_ref[...] = _unstack_heads(o, _head_mask(F32), tq)


def _mem_prompt(qm, mkb, mvb, *, seq, tq):
    n = qm.shape[0]
    nb = n // seq
    nq = seq // tq
    mem_len = mkb.shape[0] // nb
    qmap = lambda b, i: (b * nq + i, 0)
    kvmap = lambda b, i: (b, 0)
    return pl.pallas_call(
        _mem_prompt_body, grid=(nb, nq),
        in_specs=[pl.BlockSpec((tq, MEM_W), qmap), pl.BlockSpec((mem_len, MEM_W), kvmap),
                  pl.BlockSpec((mem_len, MEM_W), kvmap)],
        out_specs=pl.BlockSpec((tq, MEM_W), qmap),
        out_shape=jax.ShapeDtypeStruct((n, MEM_W), F32),
        compiler_params=_cparams("arbitrary", "arbitrary"), name="mem_prompt")(qm, mkb, mvb)


def _outproj_body(a_ref, sz_ref, m_ref, szm_ref, x_ref, w_ref, g_ref, o_ref):
    ga = (a_ref[...] * sz_ref[...]).astype(BF16)
    gm = (m_ref[...] * szm_ref[...]).astype(BF16)
    y = _dot(ga, w_ref[0:ATT_W, :]) + _dot(gm, w_ref[ATT_W:, :])
    yn = y * lax.rsqrt(jnp.mean(y * y, axis=-1, keepdims=True) + EPS) * g_ref[...]
    o_ref[...] = x_ref[...] + yn


def _outproj(a, sz, m, szm, x, w, g, *, tm):
    n = x.shape[0]
    row = lambda i: (i, 0)
    const = lambda i: (0, 0)
    return pl.pallas_call(
        _outproj_body, grid=(n // tm,),
        in_specs=[pl.BlockSpec((tm, ATT_W), row), pl.BlockSpec((tm, ATT_W), row),
                  pl.BlockSpec((tm, MEM_W), row), pl.BlockSpec((tm, MEM_W), row),
                  pl.BlockSpec((tm, D_MODEL), row), pl.BlockSpec(w.shape, const),
                  pl.BlockSpec((1, D_MODEL), const)],
        out_specs=pl.BlockSpec((tm, D_MODEL), row),
        out_shape=jax.ShapeDtypeStruct((n, D_MODEL), F32),
        compiler_params=_cparams("arbitrary"), name="outproj")(a, sz, m, szm, x, w, g.reshape(1, D_MODEL))


def _row_mask():
    lane = lax.broadcasted_iota(jnp.int32, (SUBLANE, KV_W), 1)
    sub = lax.broadcasted_iota(jnp.int32, (SUBLANE, KV_W), 0)
    return lane // HEAD_DIM == sub


def _decode_rows(q3, nj, rm):
    parts = [jnp.where(rm[None], q3[:, :, j * KV_W:(j + 1) * KV_W], 0.0) for j in range(nj)]
    return jnp.concatenate(parts, axis=1).astype(BF16)


def _decode_fold(o, nj, rm):
    outs = [jnp.sum(jnp.where(rm[None], o[:, j * SUBLANE:(j + 1) * SUBLANE, :], 0.0), axis=1, keepdims=True)
            for j in range(nj)]
    return jnp.concatenate(outs, axis=2)


def _dense_decode_body(*refs, nj, new_token):
    if new_token:
        q_ref, kn_ref, vn_ref, sink_ref, kc_ref, vc_ref, o_ref, ko_ref, vo_ref = refs
    else:
        q_ref, kc_ref, vc_ref, o_ref = refs
    rm = _row_mask()
    qr = _decode_rows(q_ref[...], nj, rm)
    kc = kc_ref[...]
    vc = vc_ref[...]
    s = jnp.einsum('brd,bld->brl', qr, kc.astype(BF16), preferred_element_type=F32)
    m = jnp.max(s, axis=-1, keepdims=True)
    if new_token:
        kn = kn_ref[...]
        vn = vn_ref[...]
        s_new = jnp.sum(qr.astype(F32) * kn.astype(BF16).astype(F32), axis=-1, keepdims=True)
        sk = sink_ref[:, 0:1][None]
        m = jnp.maximum(jnp.maximum(m, s_new), sk)
    p = jnp.exp(s - m)
    l = jnp.sum(p, axis=-1, keepdims=True)
    o = jnp.einsum('brl,bld->brd', p.astype(BF16), vc.astype(BF16), preferred_element_type=F32)
    if new_token:
        pn = jnp.exp(s_new - m)
        l = l + pn + jnp.exp(sk - m)
        o = o + pn.astype(BF16).astype(F32) * vn.astype(BF16).astype(F32)
    o_ref[...] = _decode_fold(o / l, nj, rm)
    if new_token:
        lb = kc.shape[1]
        ko_ref[:, 0:lb - 1, :] = kc[:, 1:lb, :]
        ko_ref[:, lb - 1:lb, :] = kn
        vo_ref[:, 0:lb - 1, :] = vc[:, 1:lb, :]
        vo_ref[:, lb - 1:lb, :] = vn


def _dense_decode(q3, kc, vc, *, bt, new=None):
    bd, _, qw = q3.shape
    nj = qw // KV_W
    lb = kc.shape[1]
    m3 = lambda i: (i, 0, 0)
    cspec = pl.BlockSpec((bt, lb, KV_W), m3)
    in_specs = [pl.BlockSpec((bt, 1, qw), m3)]
    args = [q3]
    out_specs = [pl.BlockSpec((bt, 1, qw), m3)]
    out_shape = [jax.ShapeDtypeStruct((bd, 1, qw), F32)]
    if new is not None:
        kn3, vn3, sink_rows = new
        in_specs += [pl.BlockSpec((bt, 1, KV_W), m3), pl.BlockSpec((bt, 1, KV_W), m3),
                     pl.BlockSpec(sink_rows.shape, lambda i: (0, 0))]
        args += [kn3, vn3, sink_rows]
        out_specs += [cspec, cspec]
        out_shape += [jax.ShapeDtypeStruct(kc.shape, F32)] * 2
    in_specs += [cspec, cspec]
    args += [kc, vc]
    body = functools.partial(_dense_decode_body, nj=nj, new_token=new is not None)
    return pl.pallas_call(
        body, grid=(bd // bt,), in_specs=in_specs, out_specs=out_specs, out_shape=out_shape,
        compiler_params=_cparams("arbitrary"), name="dense_decode")(*args)


def _fox_decode_body(pt_ref, q_ref, kn_ref, vn_ref, lfn_ref, uo_ref, kpool, vpool, lfpool, o_ref,
                     kbuf, vbuf, lfbuf, s_ref, c_ref, sems, *, n_pages, chunk):
    b = pl.program_id(0)
    nb = pl.num_programs(0)
    past = n_pages * PAGE_SIZE

    def k_copy(bb, p):
        return pltpu.make_async_copy(kpool.at[pt_ref[bb, p]], kbuf.at[pl.ds(p * PAGE_SIZE, PAGE_SIZE)], sems.at[0])

    def v_copy(bb, p):
        return pltpu.make_async_copy(vpool.at[pt_ref[bb, p]], vbuf.at[pl.ds(p * PAGE_SIZE, PAGE_SIZE)], sems.at[1])

    def lf_copy(bb, p):
        return pltpu.make_async_copy(lfpool.at[pt_ref[bb, p]], lfbuf.at[p], sems.at[2])

    def start_all(copy, bb):
        def go(p, c):
            copy(bb, p).start()
            return c
        lax.fori_loop(0, n_pages, go, 0)

    def wait_all(copy, bb):
        def go(p, c):
            copy(bb, p).wait()
            return c
        lax.fori_loop(0, n_pages, go, 0)

    @pl.when(b == 0)
    def _():
        start_all(k_copy, 0)
        start_all(lf_copy, 0)
        start_all(v_copy, 0)

    more = b + 1 < nb
    rm = _row_mask()
    qr = _decode_rows(q_ref[...], GROUP, rm)[0]

    wait_all(k_copy, b)
    for c in range(past // chunk):
        kc = kbuf[c * chunk:(c + 1) * chunk, :].astype(BF16)
        s_ref[:, c * chunk:(c + 1) * chunk] = _dot_nt(qr, kc)

    @pl.when(more)
    def _():
        start_all(k_copy, b + 1)

    wait_all(lf_copy, b)
    hi, mid, lo = _split3(lfbuf[...].reshape(n_pages * ROWS_J, PAGE_SIZE))
    uo = uo_ref[...]
    r = (_dot(hi, uo) + _dot(mid, uo) + _dot(lo, uo)).reshape(n_pages, ROWS_J, 2 * PAGE_SIZE)
    carry = jnp.zeros((ROWS_J, PAGE_SIZE), F32)
    for p in range(n_pages):
        c_ref[:, p * PAGE_SIZE:(p + 1) * PAGE_SIZE] = carry + r[p, :, :PAGE_SIZE]
        carry = carry + r[p, :, PAGE_SIZE:]

    @pl.when(more)
    def _():
        start_all(lf_copy, b + 1)

    s = s_ref[...] + ((carry[:, 0:1] + lfn_ref[...]) - c_ref[...])
    kn = kn_ref[...][0].astype(BF16).astype(F32)
    vn = vn_ref[...][0].astype(BF16).astype(F32)
    s_new = jnp.sum(qr.astype(F32) * kn, axis=-1, keepdims=True)
    m = jnp.maximum(jnp.max(s, axis=-1, keepdims=True), s_new)
    p_all = jnp.exp(s - m)
    pn = jnp.exp(s_new - m)
    l = jnp.sum(p_all, axis=-1, keepdims=True) + pn
    s_ref[...] = p_all

    wait_all(v_copy, b)
    acc = pn.astype(BF16).astype(F32) * vn
    for c in range(past // chunk):
        vc = vbuf[c * chunk:(c + 1) * chunk, :].astype(BF16)
        acc = acc + _dot(s_ref[:, c * chunk:(c + 1) * chunk].astype(BF16), vc)

    @pl.when(more)
    def _():
        start_all(v_copy, b + 1)

    o_ref[...] = _decode_fold((acc / l)[None], GROUP, rm)


def _fox_decode(page_table, q3, kn3, vn3, lfn3, uo, kpool, vpool, lfpool, *, chunk):
    bd, n_pages = page_table.shape
    past = n_pages * PAGE_SIZE
    m3 = lambda i, pt: (i, 0, 0)
    any_spec = pl.BlockSpec(memory_space=pl.ANY)
    grid_spec = pltpu.PrefetchScalarGridSpec(
        num_scalar_prefetch=1, grid=(bd,),
        in_specs=[pl.BlockSpec((1, 1, ATT_W), m3), pl.BlockSpec((1, 1, KV_W), m3),
                  pl.BlockSpec((1, 1, KV_W), m3), pl.BlockSpec((None, ROWS_J, 1), m3),
                  pl.BlockSpec(uo.shape, lambda i, pt: (0, 0)), any_spec, any_spec, any_spec],
        out_specs=pl.BlockSpec((1, 1, ATT_W), m3),
        scratch_shapes=[pltpu.VMEM((past, KV_W), F32), pltpu.VMEM((past, KV_W), F32),
                        pltpu.VMEM((n_pages, ROWS_J, PAGE_SIZE), F32),
                        pltpu.VMEM((ROWS_J, past), F32), pltpu.VMEM((ROWS_J, past), F32),
                        pltpu.SemaphoreType.DMA((3,))])
    body = functools.partial(_fox_decode_body, n_pages=n_pages, chunk=chunk)
    return pl.pallas_call(
        body, grid_spec=grid_spec, out_shape=jax.ShapeDtypeStruct((bd, 1, ATT_W), F32),
        compiler_params=_cparams("arbitrary"), name="fox_decode")(
            page_table, q3, kn3, vn3, lfn3, uo, kpool, vpool, lfpool)


def _jmajor_perm():
    idx = np.empty(ATT_W, np.int32)
    for j in range(GROUP):
        for g in range(N_KV_HEADS):
            for d in range(HEAD_DIM):
                idx[KV_W * j + HEAD_DIM * g + d] = HEAD_DIM * (GROUP * g + j) + d
    return idx


def _rows_j(x, axis):
    x = jnp.moveaxis(x, axis, -1)
    lead = x.shape[:-1]
    x = x.reshape(*lead, N_KV_HEADS, GROUP)
    x = jnp.swapaxes(x, -1, -2)
    x = jnp.pad(x, [(0, 0)] * len(lead) + [(0, 0), (0, SUBLANE - N_KV_HEADS)])
    return jnp.moveaxis(x.reshape(*lead, ROWS_J), -1, axis)


def _prep_w_in(w, with_forget):
    perm = _jmajor_perm()
    wq = w[:, 0:ATT_W][:, perm] * SCALE
    o = ATT_W
    wk = w[:, o:o + KV_W]; o += KV_W
    wv = w[:, o:o + KV_W]; o += KV_W
    wz = w[:, o:o + ATT_W][:, perm]; o += ATT_W
    wqm = w[:, o:o + MEM_W] * SCALE; o += MEM_W
    wzm = w[:, o:o + MEM_W]; o += MEM_W
    if with_forget:
        wf = jnp.pad(w[:, o:o + N_HEADS], ((0, 0), (0, LANE - N_HEADS)))
    else:
        wf = jnp.zeros((w.shape[0], LANE), w.dtype)
    return jnp.concatenate([wq, wk, wv, wz, wqm, wzm, wf], axis=1).astype(BF16)


def _rope_tables(pos, rows):
    half = ROT_DIM // 2
    inv_freq = ROPE_THETA ** (-jnp.arange(0, ROT_DIM, 2, dtype=F32) / ROT_DIM)
    ang = pos.astype(F32)[:, None] * inv_freq[None, :]
    cos, sin = jnp.cos(ang), jnp.sin(ang)
    n = pos.shape[0]
    pad = HEAD_DIM - ROT_DIM
    cos64 = jnp.concatenate([cos, cos, jnp.ones((n, pad), F32)], axis=1)
    s1_64 = jnp.concatenate([jnp.zeros((n, half), F32), sin, jnp.zeros((n, pad), F32)], axis=1)
    s2_64 = jnp.concatenate([-sin, jnp.zeros((n, half + pad), F32)], axis=1)
    tabs = [jnp.tile(t, (1, LANE // HEAD_DIM)) for t in (cos64, s1_64, s2_64)]
    return [jnp.broadcast_to(t, (rows, LANE)) if n == 1 else t for t in tabs]


def _tri_ones(n):
    i = np.arange(n)
    u = (i[:, None] <= i[None, :]).astype(np.float32)
    return jnp.asarray(np.concatenate([u, np.ones((n, n), np.float32)], axis=1), dtype=BF16)


def kernel(x_prompt, x_sample, mem_prompt, cache_fox_k, cache_fox_v, cache_fox_logf, cache_swa_k, cache_swa_v, cache_mem_k, cache_mem_v, page_table, norm_pre, norm_post, norm_mem, w_in_fox, b_forget, w_in_swa, sinks, w_mem_kv, w_out):
    nb, seq, _ = x_prompt.shape
    bd, dec_seq, _ = x_sample.shape
    assert dec_seq == 1
    depth = norm_pre.shape[0]
    mem_len = mem_prompt.shape[1]
    n_pages = page_table.shape[1]
    past = n_pages * PAGE_SIZE
    n = nb * seq
    tm = 512
    perm = _jmajor_perm()

    xp = x_prompt.reshape(n, D_MODEL)
    xs = x_sample.reshape(bd, D_MODEL)

    mk32, mv32, mkb, mvb = _memkv(mem_prompt.reshape(nb * mem_len, D_MODEL), norm_mem,
                                  w_mem_kv.astype(BF16), tm=tm)

    rope_p = _rope_tables(jnp.arange(seq), seq)
    rope_s = _rope_tables(jnp.full((1,), past), bd)
    uo_seq = _tri_ones(tm)
    uo_page = _tri_ones(PAGE_SIZE)

    fox_kp, fox_vp, fox_fp, fox_ks, fox_vs, fox_fs = [], [], [], [], [], []
    swa_kp, swa_vp, swa_ks, swa_vs = [], [], [], []
    for l in range(depth):
        j = l // 2
        is_fox = l % 2 == 0
        w_o = jnp.concatenate([w_out[l, :ATT_W][perm], w_out[l, ATT_W:]], axis=0).astype(BF16)
        if is_fox:
            w_i = _prep_w_in(w_in_fox[j], True)
            bf = jnp.pad(b_forget[j], (0, LANE - N_HEADS)).reshape(1, LANE)
            wft = _rows_j(w_in_fox[j][:, -N_HEADS:], 1).T.astype(BF16)
            bft = _rows_j(b_forget[j], 0).reshape(ROWS_J, 1)
            q, k32, v32, kb, vb, sz, qm, szm, lf, negc = _inproj(
                xp, norm_pre[l], w_i, tm=tm, seq=seq, q_dtype=BF16, b_forget=bf, cum=(wft, bft, uo_seq))
            a = _fox_flash(q, kb, vb, negc, seq=seq, tq=512, tk=512)
            fox_kp.append(k32.reshape(nb, seq, N_KV_HEADS, HEAD_DIM))
            fox_vp.append(v32.reshape(nb, seq, N_KV_HEADS, HEAD_DIM))
            fox_fp.append(lf[:, :N_HEADS].reshape(nb, seq, N_HEADS))

            qs, k32s, v32s, _, _, szs, qms, szms, lfs = _inproj(
                xs, norm_pre[l], w_i, tm=bd, seq=bd, q_dtype=F32, b_forget=bf)
            lfn3 = _rows_j(lfs[:, :N_HEADS], 1).reshape(bd, ROWS_J, 1)
            lf_pool = _rows_j(jnp.swapaxes(cache_fox_logf[j], 1, 2), 1)
            a_s = _fox_decode(page_table, qs.reshape(bd, 1, ATT_W), k32s.reshape(bd, 1, KV_W),
                              v32s.reshape(bd, 1, KV_W), lfn3, uo_page,
                              cache_fox_k[j].reshape(-1, PAGE_SIZE, KV_W),
                              cache_fox_v[j].reshape(-1, PAGE_SIZE, KV_W), lf_pool,
                              chunk=2048).reshape(bd, ATT_W)
            fox_ks.append(k32s.reshape(bd, 1, N_KV_HEADS, HEAD_DIM))
            fox_vs.append(v32s.reshape(bd, 1, N_KV_HEADS, HEAD_DIM))
            fox_fs.append(lfs[:, :N_HEADS].reshape(bd, 1, N_HEADS))
        else:
            w_i = _prep_w_in(w_in_swa[j], False)
            q, k32, v32, kb, vb, sz, qm, szm = _inproj(
                xp, norm_pre[l], w_i, tm=tm, seq=seq, q_dtype=BF16, rope_tabs=rope_p)
            sink_rows = jnp.broadcast_to(_rows_j(sinks[j], 0)[:, None], (ROWS_J, LANE))
            a = _swa_prompt(q, kb, vb, sink_rows.reshape(GROUP, SUBLANE, LANE), seq=seq, tq=512)
            buf_p = min(WINDOW, seq)
            swa_kp.append(k32.reshape(nb, seq, N_KV_HEADS, HEAD_DIM)[:, -buf_p:])
            swa_vp.append(v32.reshape(nb, seq, N_KV_HEADS, HEAD_DIM)[:, -buf_p:])

            qs, k32s, v32s, _, _, szs, qms, szms = _inproj(
                xs, norm_pre[l], w_i, tm=bd, seq=bd, q_dtype=F32, rope_tabs=rope_s)
            lb = cache_swa_k.shape[2]
            a3, kbuf, vbuf = _dense_decode(
                qs.reshape(bd, 1, ATT_W), cache_swa_k[j].reshape(bd, lb, KV_W),
                cache_swa_v[j].reshape(bd, lb, KV_W), bt=8,
                new=(k32s.reshape(bd, 1, KV_W), v32s.reshape(bd, 1, KV_W), sink_rows))
            a_s = a3.reshape(bd, ATT_W)
            swa_ks.append(kbuf.reshape(bd, lb, N_KV_HEADS, HEAD_DIM))
            swa_vs.append(vbuf.reshape(bd, lb, N_KV_HEADS, HEAD_DIM))

        mp = _mem_prompt(qm, mkb[l], mvb[l], seq=seq, tq=1024)
        ms = _dense_decode(qms.astype(F32).reshape(bd, 1, MEM_W),
                           cache_mem_k[l].reshape(bd, mem_len, MEM_W),
                           cache_mem_v[l].reshape(bd, mem_len, MEM_W), bt=8)[0].reshape(bd, MEM_W)
        xp = _outproj(a, sz, mp, szm, xp, w_o, norm_post[l], tm=tm)
        xs = _outproj(a_s, szs, ms, szms, xs, w_o, norm_post[l], tm=bd)

    kv5 = (depth, nb, mem_len, MEM_HEADS, HEAD_DIM)
    return (xp.reshape(nb, seq, D_MODEL), xs.reshape(bd, 1, D_MODEL),
            jnp.stack(fox_kp), jnp.stack(fox_vp), jnp.stack(fox_fp),
            jnp.stack(swa_kp), jnp.stack(swa_vp),
            mk32.reshape(kv5), mv32.reshape(kv5),
            jnp.stack(fox_ks), jnp.stack(fox_vs), jnp.stack(fox_fs),
            jnp.stack(swa_ks), jnp.stack(swa_vs))
```

```python
import functools
import math

import numpy as np
import jax
import jax.numpy as jnp
from jax import lax
from jax.experimental import pallas as pl
from jax.experimental.pallas import tpu as pltpu

D_MODEL = 1024
HEAD_DIM = 64
N_HEADS = 12
N_KV_HEADS = 4
GROUP = N_HEADS // N_KV_HEADS
MEM_HEADS = 4
ATT_W = N_HEADS * HEAD_DIM
KV_W = N_KV_HEADS * HEAD_DIM
MEM_W = MEM_HEADS * HEAD_DIM
ROT_DIM = HEAD_DIM // 4
ROPE_THETA = 500000.0
WINDOW = 128
PAGE_SIZE = 128
EPS = 1e-6
SCALE = HEAD_DIM ** -0.5
LOG2E = math.log2(math.e)

LANE = 128
SUBLANE = 8
VMEM_LIMIT = 56 * 1024 * 1024
NEG = -1e30
KT_TILE = LANE
ROWS_H = 2 * SUBLANE

TM_PROJ = 512
TQ_FOX = 512
TK_FOX = 512
TQ_SWA = 512
TQ_MEM = 1024
BT_DECODE = 8
CHUNK_DECODE = 2048

C_Q, C_K, C_V, C_Z, C_QM, C_ZM, C_FL = 0, 768, 1024, 1280, 2048, 2304, 2560
W_IN_PAD = C_FL + LANE

F32 = jnp.float32
BF16 = jnp.bfloat16
NT_DIMS = (((1,), (1,)), ((), ()))


def _cparams(*sem):
    return pltpu.CompilerParams(dimension_semantics=sem, vmem_limit_bytes=VMEM_LIMIT)


def _silu(z):
    return z / (1.0 + jnp.exp(-z))


def _log_sigmoid(x):
    return jnp.minimum(x, 0.0) - jnp.log(1.0 + jnp.exp(-jnp.abs(x)))


def _split3(x):
    hi = x.astype(BF16)
    r = x - hi.astype(F32)
    mid = r.astype(BF16)
    lo = (r - mid.astype(F32)).astype(BF16)
    return hi, mid, lo


def _dot(a, b):
    return jnp.dot(a, b, preferred_element_type=F32)


def _dot_nt(a, b):
    return lax.dot_general(a, b, NT_DIMS, preferred_element_type=F32)


def _bf16_round(x):
    return x.astype(BF16).astype(F32)


def _head_mask(dtype, width=KV_W):
    lane = lax.broadcasted_iota(jnp.int32, (N_KV_HEADS, 1, width), 2)
    g = lax.broadcasted_iota(jnp.int32, (N_KV_HEADS, 1, width), 0)
    return jnp.where(lane // HEAD_DIM == g, 1.0, 0.0).astype(dtype)


def _rope(x, cos, s1, s2):
    cols = []
    half = ROT_DIM // 2
    for c in range(x.shape[1] // LANE):
        xs = x[:, c * LANE:(c + 1) * LANE]
        cols.append(xs * cos + pltpu.roll(xs, half, 1) * s1 + pltpu.roll(xs, LANE - half, 1) * s2)
    return jnp.concatenate(cols, axis=1)


def _inproj_body(*refs, forget, rope, kv_rows, kv_tiles, tiles_per_seq):
    it = iter(refs)
    x_ref, g_ref, w_ref = next(it), next(it), next(it)
    if rope:
        cos_ref, s1_ref, s2_ref = next(it), next(it), next(it)
    if forget == "row":
        bf_ref = next(it)
    if forget == "cum":
        wft_ref, bft_ref, uo_ref = next(it), next(it), next(it)
    q_ref, kt32_ref, vt32_ref, sz_ref, qm_ref, szm_ref = (next(it) for _ in range(6))
    if kv_rows:
        k32_ref, v32_ref = next(it), next(it)
    if kv_tiles:
        ktb_ref, vtb_ref = next(it), next(it)
    if forget == "row":
        lf_ref = next(it)
    if forget == "cum":
        lft_ref, negc_ref, carry_ref = next(it), next(it), next(it)

    x = x_ref[...]
    tm = x.shape[0]
    h = x * lax.rsqrt(jnp.mean(x * x, axis=-1, keepdims=True) + EPS) * g_ref[...]
    hb = h.astype(BF16)

    def proj(a, b):
        return _dot(hb, w_ref[:, a:b])

    q = proj(C_Q, C_K)
    k = proj(C_K, C_V)
    v = proj(C_V, C_Z)
    if rope:
        cos, s1, s2 = cos_ref[...], s1_ref[...], s2_ref[...]
        q = _rope(q, cos, s1, s2)
        k = _rope(k, cos, s1, s2)
    q_ref[...] = q.astype(q_ref.dtype)
    kt, vt = k.T, v.T
    kt32_ref[...] = kt
    vt32_ref[...] = vt
    if kv_rows:
        k32_ref[...] = k
        v32_ref[...] = v
    if kv_tiles:
        ktb, vtb = kt.astype(BF16), vt.astype(BF16)
        for c in range(tm // KT_TILE):
            ktb_ref[c] = ktb[:, c * KT_TILE:(c + 1) * KT_TILE]
            vtb_ref[c] = vtb[:, c * KT_TILE:(c + 1) * KT_TILE]
    sz_ref[...] = _silu(proj(C_Z, C_QM))
    qm_ref[...] = proj(C_QM, C_ZM).astype(qm_ref.dtype)
    szm_ref[...] = _silu(proj(C_ZM, C_FL))
    if forget == "row":
        lf_ref[...] = _log_sigmoid(proj(C_FL, W_IN_PAD) + bf_ref[...])
    if forget == "cum":
        lft = _log_sigmoid(_dot_nt(wft_ref[...], hb) + bft_ref[...])
        hi, mid, lo = _split3(lft)
        uo = uo_ref[...]
        r = _dot(hi, uo) + _dot(mid, uo) + _dot(lo, uo)

        @pl.when(pl.program_id(0) % tiles_per_seq == 0)
        def _():
            carry_ref[...] = jnp.zeros_like(carry_ref)

        carry = carry_ref[...]
        lft_ref[...] = lft.reshape(GROUP, SUBLANE, tm)
        negc_ref[...] = (-LOG2E * (r[:, :tm] + carry)).reshape(GROUP, SUBLANE, tm)
        carry_ref[...] = carry + r[:, tm:]


def _inproj(x, g, w, *, tm, seq, q_dtype, qm_dtype, kv_rows, kv_tiles, rope_tabs=None, b_forget=None, cum=None):
    n = x.shape[0]
    nb = n // seq
    forget = "cum" if cum is not None else ("row" if b_forget is not None else "none")
    rope = rope_tabs is not None
    tps = seq // tm
    row = lambda i: (i, 0)
    const = lambda i: (0, 0)
    in_specs = [pl.BlockSpec((tm, D_MODEL), row), pl.BlockSpec((1, D_MODEL), const),
                pl.BlockSpec(w.shape, const)]
    args = [x, g.reshape(1, D_MODEL), w]
    if rope:
        tmap = (lambda i: (i % tps, 0)) if rope_tabs[0].shape[0] > tm else const
        in_specs += [pl.BlockSpec((tm, LANE), tmap)] * 3
        args += list(rope_tabs)
    if forget == "row":
        in_specs.append(pl.BlockSpec((1, LANE), const))
        args.append(b_forget)
    if forget == "cum":
        for a in cum:
            in_specs.append(pl.BlockSpec(a.shape, const))
            args.append(a)
    names, out_shape, out_specs = [], [], []

    def add(name, shape, dtype, spec):
        names.append(name)
        out_shape.append(jax.ShapeDtypeStruct(shape, dtype))
        out_specs.append(spec)

    t_spec = pl.BlockSpec((None, KV_W, tm), lambda i: (i // tps, 0, i % tps))
    add("q", (n, ATT_W), q_dtype, pl.BlockSpec((tm, ATT_W), row))
    add("kt32", (nb, KV_W, seq), F32, t_spec)
    add("vt32", (nb, KV_W, seq), F32, t_spec)
    add("sz", (n, ATT_W), F32, pl.BlockSpec((tm, ATT_W), row))
    add("qm", (n, MEM_W), qm_dtype, pl.BlockSpec((tm, MEM_W), row))
    add("szm", (n, MEM_W), F32, pl.BlockSpec((tm, MEM_W), row))
    if kv_rows:
        add("k32", (n, KV_W), F32, pl.BlockSpec((tm, KV_W), row))
        add("v32", (n, KV_W), F32, pl.BlockSpec((tm, KV_W), row))
    if kv_tiles:
        tile_spec = pl.BlockSpec((None, tm // KT_TILE, KV_W, KT_TILE), lambda i: (i // tps, i % tps, 0, 0))
        add("ktb", (nb, seq // KT_TILE, KV_W, KT_TILE), BF16, tile_spec)
        add("vtb", (nb, seq // KT_TILE, KV_W, KT_TILE), BF16, tile_spec)
    scratch = []
    if forget == "row":
        add("lf", (n, LANE), F32, pl.BlockSpec((tm, LANE), row))
    if forget == "cum":
        c_spec = pl.BlockSpec((None, GROUP, SUBLANE, tm), lambda i: (i // tps, 0, 0, i % tps))
        add("lft", (nb, GROUP, SUBLANE, seq), F32, c_spec)
        add("negc", (nb, GROUP, SUBLANE, seq), F32, c_spec)
        scratch.append(pltpu.VMEM((GROUP * SUBLANE, tm), F32))
    body = functools.partial(_inproj_body, forget=forget, rope=rope, kv_rows=kv_rows, kv_tiles=kv_tiles,
                             tiles_per_seq=tps)
    outs = pl.pallas_call(
        body, grid=(n // tm,), in_specs=in_specs, out_specs=out_specs, out_shape=out_shape,
        scratch_shapes=scratch, compiler_params=_cparams("arbitrary"), name="inproj")(*args)
    return dict(zip(names, outs))


def _memkv_body(x_ref, g_ref, w_ref, k32_ref, v32_ref, kb_ref, vb_ref):
    x = x_ref[...]
    h = x * lax.rsqrt(jnp.mean(x * x, axis=-1, keepdims=True) + EPS) * g_ref[...]
    u = _dot(h.astype(BF16), w_ref[...])
    kt, vt = u[:, :MEM_W].T, u[:, MEM_W:].T
    k32_ref[...] = kt
    v32_ref[...] = vt
    kb_ref[...] = kt.astype(BF16)
    vb_ref[...] = vt.astype(BF16)


def _memkv(mem, g, w):
    nb, mem_len, _ = mem.shape
    depth = w.shape[0]
    spec_o = pl.BlockSpec((None, None, MEM_W, mem_len), lambda l, b: (l, b, 0, 0))
    return pl.pallas_call(
        _memkv_body, grid=(depth, nb),
        in_specs=[pl.BlockSpec((None, mem_len, D_MODEL), lambda l, b: (b, 0, 0)),
                  pl.BlockSpec((None, 1, D_MODEL), lambda l, b: (l, 0, 0)),
                  pl.BlockSpec((None, D_MODEL, 2 * MEM_W), lambda l, b: (l, 0, 0))],
        out_specs=[spec_o] * 4,
        out_shape=[jax.ShapeDtypeStruct((depth, nb, MEM_W, mem_len), dt) for dt in (F32, F32, BF16, BF16)],
        compiler_params=_cparams("arbitrary", "arbitrary"), name="memkv")(mem, g.reshape(depth, 1, D_MODEL), w)


def _stack_heads(q, hm):
    return jnp.concatenate([q * hm[g] for g in range(N_KV_HEADS)], axis=0)


def _unstack_heads(o, hm, rows):
    out = o[:rows] * hm[0]
    for g in range(1, N_KV_HEADS):
        out = out + o[g * rows:(g + 1) * rows] * hm[g]
    return out


def _lane_tile(x, n):
    return jnp.concatenate([x] * n, axis=1)


def _load_tiles(ref, first, count):
    return jnp.concatenate([ref[first + c] for c in range(count)], axis=1)


def _fox_flash_body(q_ref, k_ref, v_ref, nc_ref, o_ref, q4_ref, m_ref, l_ref, acc_ref, *, tq, tk):
    q0 = pl.program_id(2) * tq
    nsub = tk // KT_TILE
    q4_ref[...] = _stack_heads(q_ref[...], _head_mask(BF16))
    m_ref[...] = jnp.full_like(m_ref, NEG)
    l_ref[...] = jnp.zeros_like(l_ref)
    acc_ref[...] = jnp.zeros_like(acc_ref)

    def step(tile0, k0, masked):
        kt = _load_tiles(k_ref, tile0, nsub)
        vt = _load_tiles(v_ref, tile0, nsub)
        s = _dot(q4_ref[...], kt).reshape(N_KV_HEADS, tq, tk)
        s = s + nc_ref[0:N_KV_HEADS, pl.ds(k0, tk)][:, None, :]
        if masked:
            row = q0 + lax.broadcasted_iota(jnp.int32, (tq, tk), 0)
            col = k0 + lax.broadcasted_iota(jnp.int32, (tq, tk), 1)
            s = jnp.where((col <= row)[None], s, NEG)
        s = s.reshape(N_KV_HEADS * tq, tk)
        m_old = m_ref[...]
        m_new = jnp.maximum(m_old, jnp.max(s, axis=-1, keepdims=True))
        alpha = jnp.exp2(m_old - m_new)
        p = jnp.exp2(s - _lane_tile(m_new, tk // LANE))
        l_ref[...] = alpha * l_ref[...] + jnp.sum(p, axis=-1, keepdims=True)
        acc_ref[...] = _lane_tile(alpha, KV_W // LANE) * acc_ref[...] + _dot_nt(p.astype(BF16), vt)
        m_ref[...] = m_new

    def full_tile(i, carry):
        step(i * nsub, pl.multiple_of(i * tk, tk), False)
        return carry

    lax.fori_loop(0, q0 // tk, full_tile, 0)
    for r in range(tq // tk):
        step((q0 + r * tk) // KT_TILE, pl.multiple_of(q0 + r * tk, tk), True)
    o = acc_ref[...] / _lane_tile(l_ref[...], KV_W // LANE)
    o_ref[...] = _unstack_heads(o, _head_mask(F32), tq)


def _fox_flash(q, ktb, vtb, negc, *, tq, tk):
    n = q.shape[0]
    nb, nt = ktb.shape[:2]
    seq = nt * KT_TILE
    nq = seq // tq
    qmap = lambda b, j, i: (b * nq + i, j)
    kvspec = pl.BlockSpec((None, nt, KV_W, KT_TILE), lambda b, j, i: (b, 0, 0, 0))
    rows = N_KV_HEADS * tq
    return pl.pallas_call(
        functools.partial(_fox_flash_body, tq=tq, tk=tk), grid=(nb, GROUP, nq),
        in_specs=[pl.BlockSpec((tq, KV_W), qmap), kvspec, kvspec,
                  pl.BlockSpec((None, None, SUBLANE, seq), lambda b, j, i: (b, j, 0, 0))],
        out_specs=pl.BlockSpec((tq, KV_W), qmap),
        out_shape=jax.ShapeDtypeStruct((n, ATT_W), F32),
        scratch_shapes=[pltpu.VMEM((rows, KV_W), BF16), pltpu.VMEM((rows, LANE), F32),
                        pltpu.VMEM((rows, LANE), F32), pltpu.VMEM((rows, KV_W), F32)],
        compiler_params=_cparams("arbitrary", "arbitrary", "arbitrary"), name="fox_flash")(q, ktb, vtb, negc)


def _swa_prompt_body(q_ref, k_ref, v_ref, sink_ref, o_ref, *, tq):
    blk = WINDOW
    q0 = pl.program_id(2) * tq
    hmb = _head_mask(BF16)
    hmf = _head_mask(F32)
    sk = sink_ref[0:N_KV_HEADS, 0:1][:, :, None]
    for r in range(tq // blk):
        t0 = q0 + r * blk
        cur = t0 // blk
        prev = jnp.maximum(cur - 1, 0)
        q4 = _stack_heads(q_ref[r * blk:(r + 1) * blk, :], hmb)
        s = jnp.concatenate([_dot(q4, k_ref[prev]), _dot(q4, k_ref[cur])], axis=1)
        s = s.reshape(N_KV_HEADS, blk, 2 * blk)
        row = t0 + lax.broadcasted_iota(jnp.int32, (blk, 2 * blk), 0)
        col = t0 - blk + lax.broadcasted_iota(jnp.int32, (blk, 2 * blk), 1)
        valid = (col >= 0) & (col <= row) & (row - col <= WINDOW)
        s = jnp.where(valid[None], s, NEG)
        m = jnp.maximum(jnp.max(s, axis=-1, keepdims=True), sk)
        p = jnp.exp(s - m)
        l = jnp.sum(p, axis=-1, keepdims=True) + jnp.exp(sk - m)
        pb = p.reshape(N_KV_HEADS * blk, 2 * blk).astype(BF16)
        pv = _dot_nt(pb[:, :blk], v_ref[prev]) + _dot_nt(pb[:, blk:], v_ref[cur])
        o = pv / l.reshape(N_KV_HEADS * blk, 1)
        o_ref[r * blk:(r + 1) * blk, :] = _unstack_heads(o, hmf, blk)


def _swa_prompt(q, ktb, vtb, sink_rows, *, tq):
    n = q.shape[0]
    nb, nt = ktb.shape[:2]
    seq = nt * KT_TILE
    nq = seq // tq
    qmap = lambda b, j, i: (b * nq + i, j)
    kvspec = pl.BlockSpec((None, nt, KV_W, KT_TILE), lambda b, j, i: (b, 0, 0, 0))
    return pl.pallas_call(
        functools.partial(_swa_prompt_body, tq=tq), grid=(nb, GROUP, nq),
        in_specs=[pl.BlockSpec((tq, KV_W), qmap), kvspec, kvspec,
                  pl.BlockSpec((None, SUBLANE, LANE), lambda b, j, i: (j, 0, 0))],
        out_specs=pl.BlockSpec((tq, KV_W), qmap),
        out_shape=jax.ShapeDtypeStruct((n, ATT_W), F32),
        compiler_params=_cparams("arbitrary", "arbitrary", "arbitrary"), name="swa_prompt")(q, ktb, vtb, sink_rows)


def _mem_prompt_body(q_ref, k_ref, v_ref, o_ref):
    tq = q_ref.shape[0]
    s = _dot(_stack_heads(q_ref[...], _head_mask(BF16)), k_ref[...])
    m = jnp.max(s, axis=-1, keepdims=True)
    p = jnp.exp(s - m)
    l = jnp.sum(p, axis=-1, keepdims=True)
    o = _dot_nt(p.astype(BF16), v_ref[...]) / l
    o_ref[...] = _unstack_heads(o, _head_mask(F32), tq)


def _mem_prompt(qm, mktb, mvtb, layer, *, seq, tq):
    n = qm.shape[0]
    nb = n // seq
    nq = seq // tq
    mem_len = mktb.shape[-1]
    qmap = lambda b, i: (b * nq + i, 0)
    kvspec = pl.BlockSpec((None, None, MEM_W, mem_len), lambda b, i: (layer, b, 0, 0))
    return pl.pallas_call(
        _mem_prompt_body, grid=(nb, nq),
        in_specs=[pl.BlockSpec((tq, MEM_W), qmap), kvspec, kvspec],
        out_specs=pl.BlockSpec((tq, MEM_W), qmap),
        out_shape=jax.ShapeDtypeStruct((n, MEM_W), F32),
        compiler_params=_cparams("arbitrary", "arbitrary"), name="mem_prompt")(qm, mktb, mvtb)


def _outproj_body(a_ref, sz_ref, m_ref, szm_ref, x_ref, w_ref, g_ref, o_ref):
    ga = (a_ref[...] * sz_ref[...]).astype(BF16)
    gm = (m_ref[...] * szm_ref[...]).astype(BF16)
    y = _dot(ga, w_ref[0:ATT_W, :]) + _dot(gm, w_ref[ATT_W:, :])
    yn = y * lax.rsqrt(jnp.mean(y * y, axis=-1, keepdims=True) + EPS) * g_ref[...]
    o_ref[...] = x_ref[...] + yn


def _outproj(a, sz, m, szm, x, w, g, *, tm):
    n = x.shape[0]
    row = lambda i: (i, 0)
    const = lambda i: (0, 0)
    return pl.pallas_call(
        _outproj_body, grid=(n // tm,),
        in_specs=[pl.BlockSpec((tm, ATT_W), row), pl.BlockSpec((tm, ATT_W), row),
                  pl.BlockSpec((tm, MEM_W), row), pl.BlockSpec((tm, MEM_W), row),
                  pl.BlockSpec((tm, D_MODEL), row), pl.BlockSpec(w.shape, const),
                  pl.BlockSpec((1, D_MODEL), const)],
        out_specs=pl.BlockSpec((tm, D_MODEL), row),
        out_shape=jax.ShapeDtypeStruct((n, D_MODEL), F32),
        compiler_params=_cparams("arbitrary"), name="outproj")(a, sz, m, szm, x, w, g.reshape(1, D_MODEL))


def _decode_masks():
    shape = (GROUP, ROWS_H, KV_W)
    j = lax.broadcasted_iota(jnp.int32, shape, 0)
    h = lax.broadcasted_iota(jnp.int32, shape, 1)
    lane = lax.broadcasted_iota(jnp.int32, shape, 2)
    ok = (h < N_HEADS) & (h % GROUP == j) & (lane // HEAD_DIM == h // GROUP)
    return jnp.where(ok, 1.0, 0.0).astype(F32)


def _decode_rows(q3, dm):
    out = q3[..., 0:1, :] * dm[0]
    for j in range(1, GROUP):
        out = out + q3[..., j:j + 1, :] * dm[j]
    return out


def _mem_row_mask():
    lane = lax.broadcasted_iota(jnp.int32, (SUBLANE, MEM_W), 1)
    sub = lax.broadcasted_iota(jnp.int32, (SUBLANE, MEM_W), 0)
    return jnp.where(lane // HEAD_DIM == sub, 1.0, 0.0).astype(F32)


def _swa_decode_body(q_ref, kn_ref, vn_ref, knt_ref, vnt_ref, sink_ref, kc_ref, vc_ref, o_ref, ko_ref, vo_ref):
    bt = q_ref.shape[0]
    b0 = pl.program_id(0) * bt
    dm = _decode_masks()
    qr = _decode_rows(q_ref[...], dm)
    qb = qr.astype(BF16)
    kc = kc_ref[...]
    vc = vc_ref[...]
    s = jnp.einsum('brd,bdl->brl', qb, kc.astype(BF16), preferred_element_type=F32)
    s_new = jnp.sum(qb.astype(F32) * _bf16_round(kn_ref[...]), axis=-1, keepdims=True)
    sk = sink_ref[:, 0:1][None]
    m = jnp.maximum(jnp.maximum(jnp.max(s, axis=-1, keepdims=True), s_new), sk)
    p = jnp.exp(s - m)
    pn = jnp.exp(s_new - m)
    l = jnp.sum(p, axis=-1, keepdims=True) + pn + jnp.exp(sk - m)
    o = jnp.einsum('brl,bdl->brd', p.astype(BF16), vc.astype(BF16), preferred_element_type=F32)
    o = (o + _bf16_round(pn) * _bf16_round(vn_ref[...])) / l
    for j in range(GROUP):
        o_ref[:, j:j + 1, :] = jnp.sum(o * dm[j], axis=1, keepdims=True)
    lb = kc.shape[2]
    last = lax.broadcasted_iota(jnp.int32, (KV_W, lb), 1) == lb - 1
    knt = knt_ref[...]
    vnt = vnt_ref[...]
    for i in range(bt):
        shift = (lb - 1) - (b0 + i)
        ko_ref[i] = jnp.where(last, pltpu.roll(knt, shift, 1), pltpu.roll(kc[i], lb - 1, 1))
        vo_ref[i] = jnp.where(last, pltpu.roll(vnt, shift, 1), pltpu.roll(vc[i], lb - 1, 1))


def _swa_decode(q3, kn3, vn3, knt, vnt, sink_rows, kcache, vcache, layer, *, bt):
    bd = q3.shape[0]
    lb = kcache.shape[-1]
    assert lb == LANE and bd == LANE
    m3 = lambda i: (i, 0, 0)
    const = lambda i: (0, 0)
    cspec = pl.BlockSpec((None, bt, KV_W, lb), lambda i: (layer, i, 0, 0))
    ospec = pl.BlockSpec((bt, KV_W, lb), m3)
    return pl.pallas_call(
        _swa_decode_body, grid=(bd // bt,),
        in_specs=[pl.BlockSpec((bt, GROUP, KV_W), m3), pl.BlockSpec((bt, 1, KV_W), m3),
                  pl.BlockSpec((bt, 1, KV_W), m3), pl.BlockSpec((KV_W, bd), const),
                  pl.BlockSpec((KV_W, bd), const), pl.BlockSpec(sink_rows.shape, const), cspec, cspec],
        out_specs=[pl.BlockSpec((bt, GROUP, KV_W), m3), ospec, ospec],
        out_shape=[jax.ShapeDtypeStruct((bd, GROUP, KV_W), F32),
                   jax.ShapeDtypeStruct((bd, KV_W, lb), F32), jax.ShapeDtypeStruct((bd, KV_W, lb), F32)],
        compiler_params=_cparams("arbitrary"), name="swa_decode")(
            q3, kn3, vn3, knt, vnt, sink_rows, kcache, vcache)


def _mem_decode_body(q_ref, kc_ref, vc_ref, o_ref):
    rm = _mem_row_mask()
    qb = (q_ref[...] * rm).astype(BF16)
    s = jnp.einsum('brd,bdl->brl', qb, kc_ref[...].astype(BF16), preferred_element_type=F32)
    m = jnp.max(s, axis=-1, keepdims=True)
    p = jnp.exp(s - m)
    l = jnp.sum(p, axis=-1, keepdims=True)
    o = jnp.einsum('brl,bdl->brd', p.astype(BF16), vc_ref[...].astype(BF16), preferred_element_type=F32) / l
    o_ref[...] = jnp.sum(o * rm, axis=1, keepdims=True)


def _mem_decode(q3, kcache, vcache, layer, *, bt):
    bd = q3.shape[0]
    lb = kcache.shape[-1]
    m3 = lambda i: (i, 0, 0)
    cspec = pl.BlockSpec((None, bt, MEM_W, lb), lambda i: (layer, i, 0, 0))
    return pl.pallas_call(
        _mem_decode_body, grid=(bd // bt,),
        in_specs=[pl.BlockSpec((bt, 1, MEM_W), m3), cspec, cspec],
        out_specs=pl.BlockSpec((bt, 1, MEM_W), m3),
        out_shape=jax.ShapeDtypeStruct((bd, 1, MEM_W), F32),
        compiler_params=_cparams("arbitrary"), name="mem_decode")(q3, kcache, vcache)


def _fox_decode_body(pt_ref, q_ref, kn_ref, vn_ref, lfn_ref, uo_ref, kpool, vpool, lfpool, o_ref,
                     kbuf, vbuf, lfbuf, s_ref, c_ref, p_ref, sems, *, layer, n_pages, chunk):
    b = pl.program_id(0)
    nb = pl.num_programs(0)
    past = n_pages * PAGE_SIZE

    def k_copy(bb, p):
        return pltpu.make_async_copy(kpool.at[layer, pt_ref[bb, p]],
                                     kbuf.at[:, pl.ds(pl.multiple_of(p * PAGE_SIZE, PAGE_SIZE), PAGE_SIZE)],
                                     sems.at[0])

    def v_copy(bb, p):
        return pltpu.make_async_copy(vpool.at[layer, pt_ref[bb, p]],
                                     vbuf.at[:, pl.ds(pl.multiple_of(p * PAGE_SIZE, PAGE_SIZE), PAGE_SIZE)],
                                     sems.at[1])

    def lf_copy(bb, p):
        return pltpu.make_async_copy(lfpool.at[layer, :, pt_ref[bb, p], :],
                                     lfbuf.at[p, pl.ds(0, N_HEADS), :], sems.at[2])

    def start_all(copy, bb):
        def go(p, c):
            copy(bb, p).start()
            return c
        lax.fori_loop(0, n_pages, go, 0)

    def wait_all(copy, bb):
        def go(p, c):
            copy(bb, p).wait()
            return c
        lax.fori_loop(0, n_pages, go, 0)

    @pl.when(b == 0)
    def _():
        p_ref[...] = jnp.zeros_like(p_ref)
        lfbuf[...] = jnp.zeros_like(lfbuf)
        start_all(k_copy, 0)
        start_all(lf_copy, 0)
        start_all(v_copy, 0)

    more = b + 1 < nb
    dm = _decode_masks()
    qr = _decode_rows(q_ref[0], dm)
    qb = qr.astype(BF16)

    wait_all(k_copy, b)
    for c in range(past // chunk):
        s_ref[:, c * chunk:(c + 1) * chunk] = _dot(qb, kbuf[:, c * chunk:(c + 1) * chunk].astype(BF16))

    @pl.when(more)
    def _():
        start_all(k_copy, b + 1)

    wait_all(lf_copy, b)
    hi, mid, lo = _split3(lfbuf[...].reshape(n_pages * ROWS_H, PAGE_SIZE))
    uo = uo_ref[...]
    r = (_dot(hi, uo) + _dot(mid, uo) + _dot(lo, uo)).reshape(n_pages, ROWS_H, 2 * PAGE_SIZE)
    carry = jnp.zeros((ROWS_H, PAGE_SIZE), F32)
    for p in range(n_pages):
        c_ref[:, p * PAGE_SIZE:(p + 1) * PAGE_SIZE] = carry + r[p, :, :PAGE_SIZE]
        carry = carry + r[p, :, PAGE_SIZE:]

    @pl.when(more)
    def _():
        start_all(lf_copy, b + 1)

    s = s_ref[...] + LOG2E * ((carry[:, 0:1] + lfn_ref[...]) - c_ref[...])
    s_new = jnp.sum(qb.astype(F32) * _bf16_round(kn_ref[0]), axis=-1, keepdims=True)
    m = jnp.maximum(jnp.max(s, axis=-1, keepdims=True), s_new)
    p_all = jnp.exp2(s - m)
    pn = jnp.exp2(s_new - m)
    l = jnp.sum(p_all, axis=-1, keepdims=True) + pn
    p_ref[0:ROWS_H, :] = p_all.astype(BF16)

    wait_all(v_copy, b)
    ot = jnp.zeros((KV_W, LANE), F32)
    for c in range(past // chunk):
        ot = ot + _dot_nt(vbuf[:, c * chunk:(c + 1) * chunk].astype(BF16), p_ref[:, c * chunk:(c + 1) * chunk])

    @pl.when(more)
    def _():
        start_all(v_copy, b + 1)

    o = (ot.T[0:ROWS_H] + _bf16_round(pn) * _bf16_round(vn_ref[0])) / l
    for j in range(GROUP):
        o_ref[0, j:j + 1, :] = jnp.sum(o * dm[j], axis=0, keepdims=True)


def _fox_decode(page_table, q3, kn3, vn3, lfn3, uo, kpool, vpool, lfpool, layer, *, chunk):
    bd, n_pages = page_table.shape
    past = n_pages * PAGE_SIZE
    m3 = lambda i, pt: (i, 0, 0)
    any_spec = pl.BlockSpec(memory_space=pl.ANY)
    grid_spec = pltpu.PrefetchScalarGridSpec(
        num_scalar_prefetch=1, grid=(bd,),
        in_specs=[pl.BlockSpec((1, GROUP, KV_W), m3), pl.BlockSpec((1, 1, KV_W), m3),
                  pl.BlockSpec((1, 1, KV_W), m3), pl.BlockSpec((None, ROWS_H, 1), m3),
                  pl.BlockSpec(uo.shape, lambda i, pt: (0, 0)), any_spec, any_spec, any_spec],
        out_specs=pl.BlockSpec((1, GROUP, KV_W), m3),
        scratch_shapes=[pltpu.VMEM((KV_W, past), F32), pltpu.VMEM((KV_W, past), F32),
                        pltpu.VMEM((n_pages, ROWS_H, PAGE_SIZE), F32),
                        pltpu.VMEM((ROWS_H, past), F32), pltpu.VMEM((ROWS_H, past), F32),
                        pltpu.VMEM((LANE, past), BF16), pltpu.SemaphoreType.DMA((3,))])
    body = functools.partial(_fox_decode_body, layer=layer, n_pages=n_pages, chunk=chunk)
    return pl.pallas_call(
        body, grid_spec=grid_spec, out_shape=jax.ShapeDtypeStruct((bd, GROUP, KV_W), F32),
        compiler_params=_cparams("arbitrary"), name="fox_decode")(
            page_table, q3, kn3, vn3, lfn3, uo, kpool, vpool, lfpool)


def _jmajor_perm():
    idx = np.empty(ATT_W, np.int32)
    for j in range(GROUP):
        for g in range(N_KV_HEADS):
            for d in range(HEAD_DIM):
                idx[KV_W * j + HEAD_DIM * g + d] = HEAD_DIM * (GROUP * g + j) + d
    return idx


def _rows_j(x, axis):
    x = jnp.moveaxis(x, axis, -1)
    lead = x.shape[:-1]
    x = x.reshape(*lead, N_KV_HEADS, GROUP)
    x = jnp.swapaxes(x, -1, -2)
    x = jnp.pad(x, [(0, 0)] * len(lead) + [(0, 0), (0, SUBLANE - N_KV_HEADS)])
    return jnp.moveaxis(x.reshape(*lead, GROUP * SUBLANE), -1, axis)


def _heads_from_rows_j(x):
    x = jnp.swapaxes(x[..., :N_KV_HEADS, :], -3, -2)
    return x.reshape(*x.shape[:-3], N_HEADS, x.shape[-1])


def _prep_w_in(w, with_forget, q_scale):
    perm = _jmajor_perm()
    wq = w[:, 0:ATT_W][:, perm] * q_scale
    o = ATT_W
    wk = w[:, o:o + KV_W]; o += KV_W
    wv = w[:, o:o + KV_W]; o += KV_W
    wz = w[:, o:o + ATT_W][:, perm]; o += ATT_W
    wqm = w[:, o:o + MEM_W] * SCALE; o += MEM_W
    wzm = w[:, o:o + MEM_W]; o += MEM_W
    if with_forget:
        wf = jnp.pad(w[:, o:o + N_HEADS], ((0, 0), (0, LANE - N_HEADS)))
    else:
        wf = jnp.zeros((w.shape[0], LANE), w.dtype)
    return jnp.concatenate([wq, wk, wv, wz, wqm, wzm, wf], axis=1).astype(BF16)


def _rope_tables(pos, rows):
    half = ROT_DIM // 2
    inv_freq = ROPE_THETA ** (-jnp.arange(0, ROT_DIM, 2, dtype=F32) / ROT_DIM)
    ang = pos.astype(F32)[:, None] * inv_freq[None, :]
    cos, sin = jnp.cos(ang), jnp.sin(ang)
    n = pos.shape[0]
    pad = HEAD_DIM - ROT_DIM
    cos64 = jnp.concatenate([cos, cos, jnp.ones((n, pad), F32)], axis=1)
    s1_64 = jnp.concatenate([jnp.zeros((n, half), F32), sin, jnp.zeros((n, pad), F32)], axis=1)
    s2_64 = jnp.concatenate([-sin, jnp.zeros((n, half + pad), F32)], axis=1)
    tabs = [jnp.tile(t, (1, LANE // HEAD_DIM)) for t in (cos64, s1_64, s2_64)]
    return [jnp.broadcast_to(t, (rows, LANE)) if n == 1 else t for t in tabs]


def _tri_ones(n):
    i = np.arange(n)
    u = (i[:, None] <= i[None, :]).astype(np.float32)
    return jnp.asarray(np.concatenate([u, np.ones((n, n), np.float32)], axis=1), dtype=BF16)


def _feature_major(x):
    xt = jnp.moveaxis(x, -3, -1)
    return xt.reshape(*xt.shape[:-3], KV_W, xt.shape[-1])


def _token_major(xt):
    x = xt.reshape(*xt.shape[:-2], N_KV_HEADS, HEAD_DIM, xt.shape[-1])
    return jnp.moveaxis(x, -1, -3)


def kernel(x_prompt, x_sample, mem_prompt, cache_fox_k, cache_fox_v, cache_fox_logf, cache_swa_k, cache_swa_v, cache_mem_k, cache_mem_v, page_table, norm_pre, norm_post, norm_mem, w_in_fox, b_forget, w_in_swa, sinks, w_mem_kv, w_out):
    nb, seq, _ = x_prompt.shape
    bd, dec_seq, _ = x_sample.shape
    assert dec_seq == 1
    depth = norm_pre.shape[0]
    n_pages = page_table.shape[1]
    past = n_pages * PAGE_SIZE
    n = nb * seq
    perm = _jmajor_perm()

    xp = x_prompt.reshape(n, D_MODEL)
    xs = x_sample.reshape(bd, D_MODEL)

    mkt32, mvt32, mktb, mvtb = _memkv(mem_prompt, norm_mem, w_mem_kv.astype(BF16))

    fox_kt = _feature_major(cache_fox_k)
    fox_vt = _feature_major(cache_fox_v)
    fox_lft = jnp.transpose(cache_fox_logf, (0, 3, 1, 2))
    swa_kt = _feature_major(cache_swa_k)
    swa_vt = _feature_major(cache_swa_v)
    mem_kt = _feature_major(cache_mem_k)
    mem_vt = _feature_major(cache_mem_v)

    rope_p = _rope_tables(jnp.arange(seq), seq)
    rope_s = _rope_tables(jnp.full((1,), past), bd)
    uo_seq = _tri_ones(TM_PROJ)
    uo_page = _tri_ones(PAGE_SIZE)

    fox_kp, fox_vp, fox_fp, fox_ks, fox_vs, fox_fs = [], [], [], [], [], []
    swa_kp, swa_vp, swa_ks, swa_vs = [], [], [], []
    for l in range(depth):
        j = l // 2
        is_fox = l % 2 == 0
        w_o = jnp.concatenate([w_out[l, :ATT_W][perm], w_out[l, ATT_W:]], axis=0).astype(BF16)
        common_p = dict(tm=TM_PROJ, seq=seq, q_dtype=BF16, qm_dtype=BF16, kv_rows=False, kv_tiles=True)
        common_s = dict(tm=bd, seq=bd, q_dtype=F32, qm_dtype=F32, kv_rows=True, kv_tiles=False)
        if is_fox:
            w_i = _prep_w_in(w_in_fox[j], True, SCALE * LOG2E)
            bf = jnp.pad(b_forget[j], (0, LANE - N_HEADS)).reshape(1, LANE)
            wft = _rows_j(w_in_fox[j][:, -N_HEADS:], 1).T.astype(BF16)
            bft = _rows_j(b_forget[j], 0).reshape(GROUP * SUBLANE, 1)
            pr = _inproj(xp, norm_pre[l], w_i, cum=(wft, bft, uo_seq), **common_p)
            a = _fox_flash(pr["q"], pr["ktb"], pr["vtb"], pr["negc"], tq=TQ_FOX, tk=TK_FOX)
            fox_kp.append(_token_major(pr["kt32"]))
            fox_vp.append(_token_major(pr["vt32"]))
            fox_fp.append(jnp.moveaxis(_heads_from_rows_j(pr["lft"]), 1, 2))

            sr = _inproj(xs, norm_pre[l], w_i, b_forget=bf, **common_s)
            lfs = sr["lf"]
            a_s = _fox_decode(page_table, sr["q"].reshape(bd, GROUP, KV_W), sr["k32"].reshape(bd, 1, KV_W),
                              sr["v32"].reshape(bd, 1, KV_W), lfs[:, :ROWS_H].reshape(bd, ROWS_H, 1), uo_page,
                              fox_kt, fox_vt, fox_lft, j, chunk=min(CHUNK_DECODE, past)).reshape(bd, ATT_W)
            fox_ks.append(jnp.moveaxis(_token_major(sr["kt32"]), 0, 1))
            fox_vs.append(jnp.moveaxis(_token_major(sr["vt32"]), 0, 1))
            fox_fs.append(lfs[:, :N_HEADS].reshape(bd, 1, N_HEADS))
        else:
            w_i = _prep_w_in(w_in_swa[j], False, SCALE)
            pr = _inproj(xp, norm_pre[l], w_i, rope_tabs=rope_p, **common_p)
            sink_rows = _rows_j(sinks[j], 0)
            sink_p = jnp.broadcast_to(sink_rows[:, None], (GROUP * SUBLANE, LANE)).reshape(GROUP, SUBLANE, LANE)
            a = _swa_prompt(pr["q"], pr["ktb"], pr["vtb"], sink_p, tq=TQ_SWA)
            buf_p = min(WINDOW, seq)
            swa_kp.append(_token_major(pr["kt32"][:, :, seq - buf_p:]))
            swa_vp.append(_token_major(pr["vt32"][:, :, seq - buf_p:]))

            sr = _inproj(xs, norm_pre[l], w_i, rope_tabs=rope_s, **common_s)
            sink_s = jnp.broadcast_to(jnp.pad(sinks[j], (0, ROWS_H - N_HEADS))[:, None], (ROWS_H, LANE))
            a3, kbuf, vbuf = _swa_decode(
                sr["q"].reshape(bd, GROUP, KV_W), sr["k32"].reshape(bd, 1, KV_W), sr["v32"].reshape(bd, 1, KV_W),
                sr["kt32"][0], sr["vt32"][0], sink_s, swa_kt, swa_vt, j, bt=BT_DECODE)
            a_s = a3.reshape(bd, ATT_W)
            swa_ks.append(_token_major(kbuf))
            swa_vs.append(_token_major(vbuf))

        mp = _mem_prompt(pr["qm"], mktb, mvtb, l, seq=seq, tq=TQ_MEM)
        ms = _mem_decode(sr["qm"].reshape(bd, 1, MEM_W), mem_kt, mem_vt, l, bt=BT_DECODE).reshape(bd, MEM_W)
        xp = _outproj(a, pr["sz"], mp, pr["szm"], xp, w_o, norm_post[l], tm=TM_PROJ)
        xs = _outproj(a_s, sr["sz"], ms, sr["szm"], xs, w_o, norm_post[l], tm=bd)

    return (xp.reshape(nb, seq, D_MODEL), xs.reshape(bd, 1, D_MODEL),
            jnp.stack(fox_kp), jnp.stack(fox_vp), jnp.stack(fox_fp),
            jnp.stack(swa_kp), jnp.stack(swa_vp),
            _token_major(mkt32), _token_major(mvt32),
            jnp.stack(fox_ks), jnp.stack(fox_vs), jnp.stack(fox_fs),
            jnp.stack(swa_ks), jnp.stack(swa_vs))
```
